```python
import math
import jax, jax.numpy as jnp
from jax import lax
import numpy as np

D_MODEL = 1024
BATCH = 8
SEQ = 4096
DEPTH = 2

HEAD_DIM = 64
A_GROUPS = ((128, 1), (512, 4), (2048, 16))
A_HEADS_PER_GROUP = 4
A_N_GROUPS = len(A_GROUPS)
A_WIDTH = A_N_GROUPS * A_HEADS_PER_GROUP * HEAD_DIM
A_OUT = A_HEADS_PER_GROUP * HEAD_DIM
BLK = 128
B_CHUNK = 128
B_GROUPS = 4
B_WIDTH = 512
B_GROUP_DIM = B_WIDTH // B_GROUPS
C_WINDOWS = (2, 4, 8, 16)
C_GROUPS = len(C_WINDOWS)
C_WIDTH = 512
C_GROUP_DIM = C_WIDTH // C_GROUPS
N_BRANCH = 3
D_FF = 2816
EPS = 1e-6

Q_OFF = 0
K_OFF = Q_OFF + A_WIDTH
V_OFF = K_OFF + A_WIDTH
UB_OFF = V_OFF + A_WIDTH
VB_OFF = UB_OFF + B_WIDTH
C_OFF = VB_OFF + B_WIDTH
G_OFF = C_OFF + C_WIDTH
IN_WIDTH = G_OFF + N_BRANCH * D_MODEL

kernel_name = "hybrid_dilated_gmlp_pool_macaron"


def rms_norm(x, g):
    xf = x.astype(jnp.float32)
    y = xf * lax.rsqrt(jnp.mean(xf * xf, axis=-1, keepdims=True) + EPS)
    return (y * g.astype(jnp.float32)).astype(x.dtype)


def layer_norm(x, g, b):
    xf = x.astype(jnp.float32)
    mu = jnp.mean(xf, axis=-1, keepdims=True)
    xc = xf - mu
    y = xc * lax.rsqrt(jnp.mean(xc * xc, axis=-1, keepdims=True) + EPS)
    return (y * g.astype(jnp.float32) + b.astype(jnp.float32)).astype(x.dtype)


def swiglu(h, w_gate, w_up, w_down):
    return (jax.nn.silu(h @ w_gate) * (h @ w_up)) @ w_down


def dilated_window_attention(q, k, v, window, dil):
    Bn, S, H, Dh = q.shape
    span = window // dil
    unit = dil * BLK
    S_pad = -(-S // unit) * unit
    L = S_pad // dil
    nb = L // BLK

    def to_strided(t):
        t = jnp.pad(t, ((0, 0), (0, S_pad - S), (0, 0), (0, 0)))
        t = t.reshape(Bn, L, dil, H, Dh).transpose(0, 2, 3, 1, 4)
        return t.reshape(Bn, dil, H, nb, BLK, Dh)

    qs, ks, vs = to_strided(q), to_strided(k), to_strided(v)
    prev = lambda t: jnp.pad(t, ((0, 0), (0, 0), (0, 0), (1, 0), (0, 0), (0, 0)))[:, :, :, :-1]
    kb = jnp.concatenate([prev(ks), ks], axis=4)
    vb = jnp.concatenate([prev(vs), vs], axis=4)

    qi = jnp.arange(BLK)[:, None] + BLK
    kj = jnp.arange(2 * BLK)[None, :]
    dist = qi - kj
    band = (dist >= 0) & (dist <= span)
    blk = jnp.arange(nb)[:, None, None]
    valid = band[None] & ~((blk == 0) & (kj[None] < BLK))

    scale = 1.0 / math.sqrt(Dh)
    s = jnp.einsum('brhnqc,brhnkc->brhnqk', qs, kb,
                   preferred_element_type=jnp.float32) * scale
    s = jnp.where(valid, s, -1e30)
    m = jnp.max(s, axis=-1, keepdims=True)
    p = jnp.exp(s - m)
    l = jnp.sum(p, axis=-1, keepdims=True)
    o = jnp.einsum('brhnqk,brhnkc->brhnqc', p, vb.astype(jnp.float32)) / l
    lse = (m + jnp.log(l))[..., 0]

    o = o.reshape(Bn, dil, H, L, Dh).transpose(0, 3, 1, 2, 4).reshape(Bn, S_pad, H, Dh)[:, :S]
    lse = lse.reshape(Bn, dil, H, L).transpose(0, 3, 1, 2).reshape(Bn, S_pad, H)[:, :S]
    return o, lse


def dilated_attention_mixture(q, k, v):
    outs, lses = [], []
    for g, (window, dil) in enumerate(A_GROUPS):
        o, lse = dilated_window_attention(q[:, :, g], k[:, :, g], v[:, :, g], window, dil)
        outs.append(o)
        lses.append(lse)
    w = jax.nn.softmax(jnp.stack(lses, axis=0), axis=0)
    o = jnp.sum(w[..., None] * jnp.stack(outs, axis=0), axis=0)
    Bn, S = o.shape[:2]
    return o.reshape(Bn, S, A_OUT).astype(q.dtype)


def spatial_gating(u, v, w_s, b_s, ln_g, ln_b):
    v = layer_norm(v, ln_g, ln_b)
    Bn, S, _ = v.shape
    nc = S // B_CHUNK
    vc = v.reshape(Bn, nc, B_CHUNK, B_GROUPS, B_GROUP_DIM)
    causal = jnp.tril(jnp.ones((B_CHUNK, B_CHUNK), dtype=bool))
    w = jnp.where(causal[None], w_s, 0.0).astype(v.dtype)
    mixed = jnp.einsum('gts,bnsgc->bntgc', w, vc) + b_s.T[None, None, :, :, None]
    return u * mixed.reshape(Bn, S, B_WIDTH)


def multiscale_pool(xc, w_c, scale_c):
    Bn, S, _ = xc.shape
    xg = xc.reshape(Bn, S, C_GROUPS, C_GROUP_DIM).astype(jnp.float32)
    cs = jnp.cumsum(xg, axis=1)
    t1 = jnp.arange(1, S + 1, dtype=jnp.float32)
    pooled = []
    for gi, w in enumerate(C_WINDOWS):
        c = cs[:, :, gi]
        shifted = jnp.pad(c, ((0, 0), (w, 0), (0, 0)))[:, :S]
        cnt = jnp.minimum(t1, float(w))[None, :, None]
        pooled.append((c - shifted) / cnt - xg[:, :, gi])
    pooled = jnp.stack(pooled, axis=2)
    y = jnp.einsum('bsgc,gcd->bsgd', pooled, w_c.astype(jnp.float32)) * scale_c.astype(jnp.float32)
    return y.reshape(Bn, S, C_WIDTH).astype(xc.dtype)


def setup_inputs(seed: int = 0) -> dict:
    key = jax.random.key(seed)
    ks = jax.random.split(key, 32)
    f32 = jnp.float32
    nrm = lambda k, shape, fan_in: jax.random.normal(k, shape, f32) * (fan_in ** -0.5)
    gain = lambda k, shape: 1.0 + 0.05 * jax.random.normal(k, shape, f32)
    small = lambda k, shape: 0.01 * jax.random.normal(k, shape, f32)
    L = DEPTH
    return {
        "x": jax.random.normal(ks[0], (BATCH, SEQ, D_MODEL), f32),
        "ffn1_norm": gain(ks[1], (L, D_MODEL)),
        "ffn1_w_gate": nrm(ks[2], (L, D_MODEL, D_FF), D_MODEL),
        "ffn1_w_up": nrm(ks[3], (L, D_MODEL, D_FF), D_MODEL),
        "ffn1_w_down": nrm(ks[4], (L, D_FF, D_MODEL), D_FF),
        "mix_norm": gain(ks[5], (L, D_MODEL)),
        "w_in": nrm(ks[6], (L, D_MODEL, IN_WIDTH), D_MODEL),
        "b_gate": small(ks[7], (L, N_BRANCH * D_MODEL)),
        "b_ln_g": gain(ks[8], (L, B_WIDTH)),
        "b_ln_b": small(ks[9], (L, B_WIDTH)),
        "b_w_s": nrm(ks[10], (L, B_GROUPS, B_CHUNK, B_CHUNK), B_CHUNK),
        "b_b_s": gain(ks[11], (L, B_GROUPS, B_CHUNK)),
        "c_w": nrm(ks[12], (L, C_GROUPS, C_GROUP_DIM, C_GROUP_DIM), C_GROUP_DIM),
        "c_scale": gain(ks[13], (L, C_GROUPS, C_GROUP_DIM)),
        "w_proj_a": nrm(ks[14], (L, A_OUT, D_MODEL), A_OUT),
        "w_proj_b": nrm(ks[15], (L, B_WIDTH, D_MODEL), B_WIDTH),
        "w_proj_c": nrm(ks[16], (L, C_WIDTH, D_MODEL), C_WIDTH),
        "w_out": nrm(ks[17], (L, D_MODEL, D_MODEL), D_MODEL),
        "ffn2_norm": gain(ks[18], (L, D_MODEL)),
        "ffn2_w_gate": nrm(ks[19], (L, D_MODEL, D_FF), D_MODEL),
        "ffn2_w_up": nrm(ks[20], (L, D_MODEL, D_FF), D_MODEL),
        "ffn2_w_down": nrm(ks[21], (L, D_FF, D_MODEL), D_FF),
        "final_norm": gain(ks[22], (D_MODEL,)),
    }


def reference(x, ffn1_norm, ffn1_w_gate, ffn1_w_up, ffn1_w_down, mix_norm, w_in, b_gate,
              b_ln_g, b_ln_b, b_w_s, b_b_s, c_w, c_scale, w_proj_a, w_proj_b, w_proj_c,
              w_out, ffn2_norm, ffn2_w_gate, ffn2_w_up, ffn2_w_down, final_norm):
    Bn, S, D = x.shape
    for i in range(DEPTH):
        x = x + 0.5 * swiglu(rms_norm(x, ffn1_norm[i]), ffn1_w_gate[i], ffn1_w_up[i], ffn1_w_down[i])

        h = rms_norm(x, mix_norm[i])
        z = h @ w_in[i]
        qkv_shape = (Bn, S, A_N_GROUPS, A_HEADS_PER_GROUP, HEAD_DIM)
        q = z[..., Q_OFF:K_OFF].reshape(qkv_shape)
        k = z[..., K_OFF:V_OFF].reshape(qkv_shape)
        v = z[..., V_OFF:UB_OFF].reshape(qkv_shape)
        y_a = dilated_attention_mixture(q, k, v)

        uv = jax.nn.gelu(z[..., UB_OFF:C_OFF], approximate=False)
        y_b = spatial_gating(uv[..., :B_WIDTH], uv[..., B_WIDTH:], b_w_s[i], b_b_s[i],
                             b_ln_g[i], b_ln_b[i])

        y_c = multiscale_pool(z[..., C_OFF:G_OFF], c_w[i], c_scale[i])

        gates = jax.nn.sigmoid(z[..., G_OFF:] + b_gate[i]).reshape(Bn, S, N_BRANCH, D)
        merged = (gates[:, :, 0] * (y_a @ w_proj_a[i])
                  + gates[:, :, 1] * (y_b @ w_proj_b[i])
                  + gates[:, :, 2] * (y_c @ w_proj_c[i]))
        x = x + merged @ w_out[i]

        x = x + 0.5 * swiglu(rms_norm(x, ffn2_norm[i]), ffn2_w_gate[i], ffn2_w_up[i], ffn2_w_down[i])
    return rms_norm(x, final_norm)
```

```python
import functools
import math

import jax
import jax.numpy as jnp
from jax import lax
from jax.experimental import pallas as pl
from jax.experimental.pallas import tpu as pltpu

F32 = jnp.float32
BF16 = jnp.bfloat16

D_MODEL = 1024
D_FF = 2816
DEPTH = 2
EPS = 1e-6

HEAD_DIM = 64
A_GROUPS = ((128, 1), (512, 4), (2048, 16))
A_HEADS = 4
A_GROUP_W = A_HEADS * HEAD_DIM
A_WIDTH = len(A_GROUPS) * A_GROUP_W
BLK = 128
B_CHUNK = 128
B_GROUPS = 4
B_WIDTH = 512
C_WINDOWS = (2, 4, 8, 16)
C_WIDTH = 512
C_GROUP_DIM = 128
C_HALO = 16

Q_OFF = 0
K_OFF = A_WIDTH
V_OFF = 2 * A_WIDTH
UB_OFF = 3 * A_WIDTH
VB_OFF = UB_OFF + B_WIDTH
C_OFF = VB_OFF + B_WIDTH
G_OFF = C_OFF + C_WIDTH

TOKEN_TILE = 512
VMEM_LIMIT_BYTES = 56 * 1024 * 1024
NEG_INF = -1e30


def _const_spec(shape):
    zeros = (0,) * len(shape)
    return pl.BlockSpec(shape, lambda *_: zeros, pipeline_mode=pl.Buffered(1))


def _rms_norm(x, g):
    return x * lax.rsqrt(jnp.mean(x * x, axis=-1, keepdims=True) + EPS) * g


def _gelu(x):
    return 0.5 * x * (1.0 + lax.erf(x * math.sqrt(0.5)))


def _swiglu_half_step(x, g, wg_ref, wu_ref, wd_ref):
    h = _rms_norm(x, g).astype(BF16)
    gate = jnp.dot(h, wg_ref[...], preferred_element_type=F32)
    up = jnp.dot(h, wu_ref[...], preferred_element_type=F32)
    act = (gate * jax.nn.sigmoid(gate) * up).astype(BF16)
    return x + 0.5 * jnp.dot(act, wd_ref[...], preferred_element_type=F32)


def _ffn_kernel(x_ref, g_ref, wg_ref, wu_ref, wd_ref, o_ref):
    o_ref[...] = _swiglu_half_step(x_ref[...], g_ref[...], wg_ref, wu_ref, wd_ref)


def _ffn(x, g, wg, wu, wd):
    n = x.shape[0]
    tile = pl.BlockSpec((TOKEN_TILE, D_MODEL), lambda i: (i, 0))
    return pl.pallas_call(
        _ffn_kernel,
        grid=(n // TOKEN_TILE,),
        in_specs=[tile, _const_spec((1, D_MODEL)), _const_spec((D_MODEL, D_FF)),
                  _const_spec((D_MODEL, D_FF)), _const_spec((D_FF, D_MODEL))],
        out_specs=tile,
        out_shape=jax.ShapeDtypeStruct((n, D_MODEL), F32),
        compiler_params=pltpu.CompilerParams(
            dimension_semantics=("arbitrary",), vmem_limit_bytes=VMEM_LIMIT_BYTES),
        name="ffn",
    )(x, g, wg, wu, wd)


def _inproj_kernel(x_ref, g_ref, win_ref, bgate_ref, lng_ref, lnb_ref, ws_ref, bs_ref,
                   cw_ref, cs_ref, wpb_ref, wpc_ref,
                   q_ref, k_ref, v_ref, g0_ref, part_ref, cext_ref):
    tm = x_ref.shape[0]
    h = _rms_norm(x_ref[...], g_ref[...]).astype(BF16)

    def proj(lo, hi):
        return jnp.dot(h, win_ref[:, lo:hi], preferred_element_type=F32)

    q_ref[...] = (proj(Q_OFF, K_OFF) * (1.0 / math.sqrt(HEAD_DIM))).astype(BF16)
    k_ref[...] = proj(K_OFF, V_OFF).astype(BF16)
    v_ref[...] = proj(V_OFF, UB_OFF).astype(BF16)

    u = _gelu(proj(UB_OFF, VB_OFF))
    vb = _gelu(proj(VB_OFF, C_OFF))
    mu = jnp.mean(vb, axis=-1, keepdims=True)
    vc = vb - mu
    vn = vc * lax.rsqrt(jnp.mean(vc * vc, axis=-1, keepdims=True) + EPS)
    vn = (vn * lng_ref[...] + lnb_ref[...]).astype(BF16)
    row = lax.broadcasted_iota(jnp.int32, (B_CHUNK, B_CHUNK), 0)
    col = lax.broadcasted_iota(jnp.int32, (B_CHUNK, B_CHUNK), 1)
    causal = col <= row
    gd = B_WIDTH // B_GROUPS
    mixed_groups = []
    for gi in range(B_GROUPS):
        w = jnp.where(causal, ws_ref[gi], 0.0).astype(BF16)
        bias = bs_ref[:, gi:gi + 1]
        chunks = []
        for n in range(tm // B_CHUNK):
            vs = vn[n * B_CHUNK:(n + 1) * B_CHUNK, gi * gd:(gi + 1) * gd]
            chunks.append(jnp.dot(w, vs, preferred_element_type=F32) + bias)
        mixed_groups.append(jnp.concatenate(chunks, axis=0))
    y_b = (u * jnp.concatenate(mixed_groups, axis=1)).astype(BF16)

    zc = proj(C_OFF, G_OFF)
    s_idx = pl.program_id(1)

    @pl.when(s_idx == 0)
    def _():
        cext_ref[0:C_HALO, :] = jnp.zeros((C_HALO, C_WIDTH), F32)

    cext_ref[C_HALO:C_HALO + tm, :] = zc
    t1 = (s_idx * tm + 1 + lax.broadcasted_iota(jnp.int32, (tm, 1), 0)).astype(F32)
    yc_groups = []
    for gi, w in enumerate(C_WINDOWS):
        cols = slice(gi * C_GROUP_DIM, (gi + 1) * C_GROUP_DIM)
        cur = zc[:, cols]
        acc = cur
        for j in range(1, w):
            acc = acc + cext_ref[C_HALO - j:C_HALO - j + tm, cols]
        pooled = acc * (1.0 / jnp.minimum(t1, float(w))) - cur
        y = jnp.dot(pooled.astype(BF16), cw_ref[gi], preferred_element_type=F32)
        yc_groups.append(y * cs_ref[:, cols])
    y_c = jnp.concatenate(yc_groups, axis=1).astype(BF16)
    cext_ref[0:C_HALO, :] = zc[tm - C_HALO:tm, :]

    def gate(i):
        lo = G_OFF + i * D_MODEL
        return jax.nn.sigmoid(proj(lo, lo + D_MODEL) + bgate_ref[:, i * D_MODEL:(i + 1) * D_MODEL])

    g0_ref[...] = gate(0)
    part = gate(1) * jnp.dot(y_b, wpb_ref[...], preferred_element_type=F32)
    part_ref[...] = part + gate(2) * jnp.dot(y_c, wpc_ref[...], preferred_element_type=F32)


def _inproj(x, g, win, bgate, lng, lnb, ws, bs_t, cw, cs, wpb, wpc):
    bn, s, _ = x.shape
    tm = TOKEN_TILE

    def tile(width):
        return pl.BlockSpec((None, tm, width), lambda b, i: (b, i, 0))

    in_width = win.shape[1]
    return pl.pallas_call(
        _inproj_kernel,
        grid=(bn, s // tm),
        in_specs=[tile(D_MODEL), _const_spec((1, D_MODEL)), _const_spec((D_MODEL, in_width)),
                  _const_spec((1, 3 * D_MODEL)), _const_spec((1, B_WIDTH)),
                  _const_spec((1, B_WIDTH)), _const_spec((B_GROUPS, B_CHUNK, B_CHUNK)),
                  _const_spec((B_CHUNK, B_GROUPS)),
                  _const_spec((len(C_WINDOWS), C_GROUP_DIM, C_GROUP_DIM)),
                  _const_spec((1, C_WIDTH)), _const_spec((B_WIDTH, D_MODEL)),
                  _const_spec((C_WIDTH, D_MODEL))],
        out_specs=[tile(A_WIDTH), tile(A_WIDTH), tile(A_WIDTH), tile(D_MODEL), tile(D_MODEL)],
        out_shape=[jax.ShapeDtypeStruct((bn, s, A_WIDTH), BF16)] * 3
        + [jax.ShapeDtypeStruct((bn, s, D_MODEL), F32)] * 2,
        scratch_shapes=[pltpu.VMEM((C_HALO + tm, C_WIDTH), F32)],
        compiler_params=pltpu.CompilerParams(
            dimension_semantics=("arbitrary", "arbitrary"),
            vmem_limit_bytes=VMEM_LIMIT_BYTES),
        name="inproj",
    )(x, g, win, bgate, lng, lnb, ws, bs_t, cw, cs, wpb, wpc)


def _attn_kernel(q_ref, kp_ref, kc_ref, vp_ref, vc_ref, o_ref, lse_ref):
    tq = q_ref.shape[0]
    first_key = jnp.where(pl.program_id(2) == 0, BLK, 0)
    lane = lax.broadcasted_iota(jnp.int32, (BLK, 2 * HEAD_DIM), 1)
    low_lanes = lane < HEAD_DIM
    qi = lax.broadcasted_iota(jnp.int32, (2 * BLK, 2 * BLK), 0) % BLK
    kj = lax.broadcasted_iota(jnp.int32, (2 * BLK, 2 * BLK), 1)
    dist = qi + BLK - kj
    band = (dist >= 0) & (dist <= BLK)
    for j in range(tq // BLK):
        rows = slice(j * BLK, (j + 1) * BLK)
        valid = band
        if j == 0:
            valid = band & (kj >= first_key)
        for hp in range(A_HEADS // 2):
            cols = slice(hp * 2 * HEAD_DIM, (hp + 1) * 2 * HEAD_DIM)
            q2 = q_ref[rows, cols]
            zero = jnp.zeros_like(q2)
            qs = jnp.concatenate([jnp.where(low_lanes, q2, zero),
                                  jnp.where(low_lanes, zero, q2)], axis=0)
            if j == 0:
                kcat = jnp.concatenate([kp_ref[:, cols], kc_ref[rows, cols]], axis=0)
                vcat = jnp.concatenate([vp_ref[:, cols], vc_ref[rows, cols]], axis=0)
            else:
                both = slice((j - 1) * BLK, (j + 1) * BLK)
                kcat = kc_ref[both, cols]
                vcat = vc_ref[both, cols]
            s = lax.dot_general(qs, kcat, (((1,), (1,)), ((), ())),
                                preferred_element_type=F32)
            s = jnp.where(valid, s, NEG_INF)
            m = jnp.max(s, axis=-1, keepdims=True)
            p = jnp.exp(s - m)
            l = jnp.sum(p, axis=-1, keepdims=True)
            o2 = jnp.dot(p.astype(BF16), vcat, preferred_element_type=F32) / l
            lse2 = m + jnp.log(l)
            o_ref[rows, cols] = jnp.where(low_lanes, o2[:BLK], o2[BLK:])
            lse_ref[rows, cols] = jnp.where(low_lanes, lse2[:BLK], lse2[BLK:])


def _attn_group(q, k, v, gi, dil):
    bn, s, _ = q.shape
    length = s // dil
    tq = min(length, 4 * BLK)
    per_tile = tq // BLK
    ncol = A_WIDTH // A_GROUP_W
    strided = lambda t: t.reshape(bn, length, dil * A_WIDTH)
    cur = pl.BlockSpec((None, tq, A_GROUP_W), lambda b, r, n: (b, n, r * ncol + gi))
    prev = pl.BlockSpec((None, BLK, A_GROUP_W),
                        lambda b, r, n: (b, jnp.maximum(n * per_tile - 1, 0), r * ncol + gi))
    out = pl.BlockSpec((None, tq, A_GROUP_W), lambda b, r, n: (b, n, r))
    o, lse = pl.pallas_call(
        _attn_kernel,
        grid=(bn, dil, length // tq),
        in_specs=[cur, prev, cur, prev, cur],
        out_specs=[out, out],
        out_shape=[jax.ShapeDtypeStruct((bn, length, dil * A_GROUP_W), F32)] * 2,
        compiler_params=pltpu.CompilerParams(
            dimension_semantics=("arbitrary", "arbitrary", "arbitrary")),
        name=f"attn_d{dil}",
    )(strided(q), strided(k), strided(k), strided(v), strided(v))
    return o.reshape(bn, s, A_GROUP_W), lse.reshape(bn, s, A_GROUP_W)


def _merge_kernel(x_ref, o0_ref, o1_ref, o2_ref, l0_ref, l1_ref, l2_ref, g0_ref, part_ref,
                  wpa_ref, wout_ref, g_ref, wg_ref, wu_ref, wd_ref, fin_ref, o_ref, *, final):
    l0, l1, l2 = l0_ref[...], l1_ref[...], l2_ref[...]
    mx = jnp.maximum(jnp.maximum(l0, l1), l2)
    e0, e1, e2 = jnp.exp(l0 - mx), jnp.exp(l1 - mx), jnp.exp(l2 - mx)
    y_a = (e0 * o0_ref[...] + e1 * o1_ref[...] + e2 * o2_ref[...]) / (e0 + e1 + e2)
    merged = g0_ref[...] * jnp.dot(y_a.astype(BF16), wpa_ref[...], preferred_element_type=F32)
    merged = (merged + part_ref[...]).astype(BF16)
    x = x_ref[...] + jnp.dot(merged, wout_ref[...], preferred_element_type=F32)
    x = _swiglu_half_step(x, g_ref[...], wg_ref, wu_ref, wd_ref)
    if final:
        x = _rms_norm(x, fin_ref[...])
    o_ref[...] = x


def _merge(x, os, lses, g0, part, wpa, wout, g, wg, wu, wd, fin, final):
    n = x.shape[0]
    tm = TOKEN_TILE
    tile = lambda width: pl.BlockSpec((tm, width), lambda i: (i, 0))
    return pl.pallas_call(
        functools.partial(_merge_kernel, final=final),
        grid=(n // tm,),
        in_specs=[tile(D_MODEL)] + [tile(A_GROUP_W)] * 6 + [tile(D_MODEL)] * 2
        + [_const_spec((A_GROUP_W, D_MODEL)), _const_spec((D_MODEL, D_MODEL)),
           _const_spec((1, D_MODEL)), _const_spec((D_MODEL, D_FF)),
           _const_spec((D_MODEL, D_FF)), _const_spec((D_FF, D_MODEL)),
           _const_spec((1, D_MODEL))],
        out_specs=tile(D_MODEL),
        out_shape=jax.ShapeDtypeStruct((n, D_MODEL), F32),
        compiler_params=pltpu.CompilerParams(
            dimension_semantics=("arbitrary",), vmem_limit_bytes=VMEM_LIMIT_BYTES),
        name="merge_ffn",
    )(x, *os, *lses, g0, part, wpa, wout, g, wg, wu, wd, fin)


def kernel(x, ffn1_norm, ffn1_w_gate, ffn1_w_up, ffn1_w_down, mix_norm, w_in, b_gate,
           b_ln_g, b_ln_b, b_w_s, b_b_s, c_w, c_scale, w_proj_a, w_proj_b, w_proj_c,
           w_out, ffn2_norm, ffn2_w_gate, ffn2_w_up, ffn2_w_down, final_norm):
    bn, s, d = x.shape
    n = bn * s
    bf = lambda w: w.astype(BF16)
    row = lambda p: p.reshape(1, -1)
    x = x.reshape(n, d)
    for i in range(DEPTH):
        x = _ffn(x, row(ffn1_norm[i]), bf(ffn1_w_gate[i]), bf(ffn1_w_up[i]), bf(ffn1_w_down[i]))
        q, k, v, g0, part = _inproj(
            x.reshape(bn, s, d), row(mix_norm[i]), bf(w_in[i]), row(b_gate[i]),
            row(b_ln_g[i]), row(b_ln_b[i]), b_w_s[i], b_b_s[i].T, bf(c_w[i]),
            row(c_scale[i]), bf(w_proj_b[i]), bf(w_proj_c[i]))
        os, lses = [], []
        for gi, (_, dil) in enumerate(A_GROUPS):
            o, lse = _attn_group(q, k, v, gi, dil)
            os.append(o.reshape(n, A_GROUP_W))
            lses.append(lse.reshape(n, A_GROUP_W))
        x = _merge(x, os, lses, g0.reshape(n, d), part.reshape(n, d), bf(w_proj_a[i]),
                   bf(w_out[i]), row(ffn2_norm[i]), bf(ffn2_w_gate[i]), bf(ffn2_w_up[i]),
                   bf(ffn2_w_down[i]), row(final_norm), final=(i == DEPTH - 1))
    return x.reshape(bn, s, d)
```

```python
import functools
import math

import jax
import jax.numpy as jnp
from jax import lax
from jax.experimental import pallas as pl
from jax.experimental.pallas import tpu as pltpu

F32 = jnp.float32
BF16 = jnp.bfloat16

D_MODEL = 1024
D_FF = 2816
DEPTH = 2
EPS = 1e-6

LANES = 128
HEAD_DIM = 64
A_GROUPS = ((128, 1), (512, 4), (2048, 16))
A_HEADS = 4
A_GROUP_W = A_HEADS * HEAD_DIM
A_SLABS = A_GROUP_W // LANES
A_WIDTH = len(A_GROUPS) * A_GROUP_W
BLK = 128
B_CHUNK = 128
B_GROUPS = 4
B_WIDTH = 512
C_WINDOWS = (2, 4, 8, 16)
C_WIDTH = 512
C_GROUP_DIM = 128
C_HALO = 16

Q_OFF = 0
K_OFF = A_WIDTH
V_OFF = 2 * A_WIDTH
UB_OFF = 3 * A_WIDTH
VB_OFF = UB_OFF + B_WIDTH
C_OFF = VB_OFF + B_WIDTH
G_OFF = C_OFF + C_WIDTH
IN_WIDTH = G_OFF + 3 * D_MODEL

TOKEN_TILE = 512
VMEM_LIMIT_BYTES = 56 * 1024 * 1024
NEG_INF = -1e30


def _layer_spec(layer, shape):
    zeros = (0,) * len(shape)
    return pl.BlockSpec((None,) + shape, lambda *_: (layer,) + zeros,
                        pipeline_mode=pl.Buffered(1))


def _token_spec(width):
    return pl.BlockSpec((None, TOKEN_TILE, width), lambda b, i: (b, i, 0))


def _rms_norm(x, g):
    return x * lax.rsqrt(jnp.mean(x * x, axis=-1, keepdims=True) + EPS) * g


def _gelu(x):
    return 0.5 * x * (1.0 + lax.erf(x * math.sqrt(0.5)))


def _swiglu_half_step(x, g, wg_ref, wu_ref, wd_ref):
    h = _rms_norm(x, g).astype(BF16)
    gate = jnp.dot(h, wg_ref[...], preferred_element_type=F32)
    up = jnp.dot(h, wu_ref[...], preferred_element_type=F32)
    act = (gate * jax.nn.sigmoid(gate) * up).astype(BF16)
    return x + 0.5 * jnp.dot(act, wd_ref[...], preferred_element_type=F32)


def _ffn_kernel(x_ref, g_ref, wg_ref, wu_ref, wd_ref, o_ref):
    o_ref[...] = _swiglu_half_step(x_ref[...], g_ref[...], wg_ref, wu_ref, wd_ref)


def _ffn(layer, x, g, wg, wu, wd):
    bn, s, _ = x.shape
    return pl.pallas_call(
        _ffn_kernel,
        grid=(bn, s // TOKEN_TILE),
        in_specs=[_token_spec(D_MODEL), _layer_spec(layer, (1, D_MODEL)),
                  _layer_spec(layer, (D_MODEL, D_FF)), _layer_spec(layer, (D_MODEL, D_FF)),
                  _layer_spec(layer, (D_FF, D_MODEL))],
        out_specs=_token_spec(D_MODEL),
        out_shape=jax.ShapeDtypeStruct(x.shape, F32),
        compiler_params=pltpu.CompilerParams(
            dimension_semantics=("arbitrary", "arbitrary"),
            vmem_limit_bytes=VMEM_LIMIT_BYTES),
        name="ffn",
    )(x, g, wg, wu, wd)


def _inproj_kernel(x_ref, g_ref, win_ref, bgate_ref, lng_ref, lnb_ref, ws_ref, bs_ref,
                   cw_ref, cs_ref, wpb_ref, wpc_ref,
                   qkv0_ref, qkv1_ref, qkv2_ref, g0_ref, part_ref, cext_ref, dil_ref):
    tm = x_ref.shape[0]
    h = _rms_norm(x_ref[...], g_ref[...]).astype(BF16)

    def proj(lo, hi):
        return jnp.dot(h, win_ref[:, lo:hi], preferred_element_type=F32)

    operands = ((Q_OFF, 1.0 / math.sqrt(HEAD_DIM)), (K_OFF, 1.0), (V_OFF, 1.0))
    for which, (off, scale) in enumerate(operands):
        for gi, out_ref in enumerate((qkv0_ref, qkv1_ref, qkv2_ref)):
            dil = A_GROUPS[gi][1]
            z = proj(off + gi * A_GROUP_W, off + (gi + 1) * A_GROUP_W)
            if scale != 1.0:
                z = z * scale
            if dil == 1:
                out_ref[which, 0] = z.astype(BF16)
                continue
            for slab in range(A_SLABS):
                dil_ref[slab] = z[:, slab * LANES:(slab + 1) * LANES]
            for r in range(dil):
                for slab in range(A_SLABS):
                    out_ref[which, r, :, slab * LANES:(slab + 1) * LANES] = (
                        dil_ref[slab, pl.ds(r, tm // dil, stride=dil), :].astype(BF16))

    u = _gelu(proj(UB_OFF, VB_OFF))
    vb = _gelu(proj(VB_OFF, C_OFF))
    mu = jnp.mean(vb, axis=-1, keepdims=True)
    vc = vb - mu
    vn = vc * lax.rsqrt(jnp.mean(vc * vc, axis=-1, keepdims=True) + EPS)
    vn = (vn * lng_ref[...] + lnb_ref[...]).astype(BF16)
    row = lax.broadcasted_iota(jnp.int32, (B_CHUNK, B_CHUNK), 0)
    col = lax.broadcasted_iota(jnp.int32, (B_CHUNK, B_CHUNK), 1)
    causal = col <= row
    gd = B_WIDTH // B_GROUPS
    mixed_groups = []
    for gi in range(B_GROUPS):
        w = jnp.where(causal, ws_ref[gi], 0.0).astype(BF16)
        bias = bs_ref[:, gi:gi + 1]
        chunks = []
        for n in range(tm // B_CHUNK):
            vs = vn[n * B_CHUNK:(n + 1) * B_CHUNK, gi * gd:(gi + 1) * gd]
            chunks.append(jnp.dot(w, vs, preferred_element_type=F32) + bias)
        mixed_groups.append(jnp.concatenate(chunks, axis=0))
    y_b = (u * jnp.concatenate(mixed_groups, axis=1)).astype(BF16)

    zc = proj(C_OFF, G_OFF)
    s_idx = pl.program_id(1)

    @pl.when(s_idx == 0)
    def _():
        cext_ref[0:C_HALO, :] = jnp.zeros((C_HALO, C_WIDTH), F32)

    cext_ref[C_HALO:C_HALO + tm, :] = zc
    t1 = (s_idx * tm + 1 + lax.broadcasted_iota(jnp.int32, (tm, 1), 0)).astype(F32)
    yc_groups = []
    for gi, w in enumerate(C_WINDOWS):
        cols = slice(gi * C_GROUP_DIM, (gi + 1) * C_GROUP_DIM)
        cur = zc[:, cols]
        acc = cur
        for j in range(1, w):
            acc = acc + cext_ref[C_HALO - j:C_HALO - j + tm, cols]
        pooled = acc * (1.0 / jnp.minimum(t1, float(w))) - cur
        y = jnp.dot(pooled.astype(BF16), cw_ref[gi], preferred_element_type=F32)
        yc_groups.append(y * cs_ref[:, cols])
    y_c = jnp.concatenate(yc_groups, axis=1).astype(BF16)
    cext_ref[0:C_HALO, :] = zc[tm - C_HALO:tm, :]

    def gate(i):
        lo = G_OFF + i * D_MODEL
        return jax.nn.sigmoid(proj(lo, lo + D_MODEL) + bgate_ref[:, i * D_MODEL:(i + 1) * D_MODEL])

    g0_ref[...] = gate(0)
    part = gate(1) * jnp.dot(y_b, wpb_ref[...], preferred_element_type=F32)
    part_ref[...] = part + gate(2) * jnp.dot(y_c, wpc_ref[...], preferred_element_type=F32)


def _inproj(layer, x, g, win, bgate, lng, lnb, ws, bs_t, cw, cs, wpb, wpc):
    bn, s, _ = x.shape
    tm = TOKEN_TILE
    qkv_specs, qkv_shapes = [], []
    for _, dil in A_GROUPS:
        qkv_specs.append(pl.BlockSpec((None, 3, dil, tm // dil, A_GROUP_W),
                                      lambda b, i: (b, 0, 0, i, 0)))
        qkv_shapes.append(jax.ShapeDtypeStruct((bn, 3, dil, s // dil, A_GROUP_W), BF16))
    return pl.pallas_call(
        _inproj_kernel,
        grid=(bn, s // tm),
        in_specs=[_token_spec(D_MODEL), _layer_spec(layer, (1, D_MODEL)),
                  _layer_spec(layer, (D_MODEL, IN_WIDTH)), _layer_spec(layer, (1, 3 * D_MODEL)),
                  _layer_spec(layer, (1, B_WIDTH)), _layer_spec(layer, (1, B_WIDTH)),
                  _layer_spec(layer, (B_GROUPS, B_CHUNK, B_CHUNK)),
                  _layer_spec(layer, (B_CHUNK, B_GROUPS)),
                  _layer_spec(layer, (len(C_WINDOWS), C_GROUP_DIM, C_GROUP_DIM)),
                  _layer_spec(layer, (1, C_WIDTH)), _layer_spec(layer, (B_WIDTH, D_MODEL)),
                  _layer_spec(layer, (C_WIDTH, D_MODEL))],
        out_specs=qkv_specs + [_token_spec(D_MODEL), _token_spec(D_MODEL)],
        out_shape=qkv_shapes + [jax.ShapeDtypeStruct((bn, s, D_MODEL), F32)] * 2,
        scratch_shapes=[pltpu.VMEM((C_HALO + tm, C_WIDTH), F32),
                        pltpu.VMEM((A_SLABS, tm, LANES), F32)],
        compiler_params=pltpu.CompilerParams(
            dimension_semantics=("arbitrary", "arbitrary"),
            vmem_limit_bytes=VMEM_LIMIT_BYTES),
        name="inproj",
    )(x, g, win, bgate, lng, lnb, ws, bs_t, cw, cs, wpb, wpc)


def _attn_kernel(q_ref, kp_ref, kc_ref, vp_ref, vc_ref, o_ref, lse_ref):
    tq = q_ref.shape[0]
    first_key = jnp.where(pl.program_id(2) == 0, BLK, 0)
    lane = lax.broadcasted_iota(jnp.int32, (BLK, 2 * HEAD_DIM), 1)
    low_lanes = lane < HEAD_DIM
    qi = lax.broadcasted_iota(jnp.int32, (2 * BLK, 2 * BLK), 0) % BLK
    kj = lax.broadcasted_iota(jnp.int32, (2 * BLK, 2 * BLK), 1)
    dist = qi + BLK - kj
    band = (dist >= 0) & (dist <= BLK)
    for j in range(tq // BLK):
        rows = slice(j * BLK, (j + 1) * BLK)
        valid = band
        if j == 0:
            valid = band & (kj >= first_key)
        for hp in range(A_HEADS // 2):
            cols = slice(hp * 2 * HEAD_DIM, (hp + 1) * 2 * HEAD_DIM)
            q2 = q_ref[rows, cols]
            zero = jnp.zeros_like(q2)
            qs = jnp.concatenate([jnp.where(low_lanes, q2, zero),
                                  jnp.where(low_lanes, zero, q2)], axis=0)
            if j == 0:
                kcat = jnp.concatenate([kp_ref[:, cols], kc_ref[rows, cols]], axis=0)
                vcat = jnp.concatenate([vp_ref[:, cols], vc_ref[rows, cols]], axis=0)
            else:
                both = slice((j - 1) * BLK, (j + 1) * BLK)
                kcat = kc_ref[both, cols]
                vcat = vc_ref[both, cols]
            s = lax.dot_general(qs, kcat, (((1,), (1,)), ((), ())),
                                preferred_element_type=F32)
            s = jnp.where(valid, s, NEG_INF)
            m = jnp.max(s, axis=-1, keepdims=True)
            p = jnp.exp(s - m)
            l = jnp.sum(p, axis=-1, keepdims=True)
            o2 = jnp.dot(p.astype(BF16), vcat, preferred_element_type=F32) / l
            lse2 = m + jnp.log(l)
            o_ref[rows, cols] = jnp.where(low_lanes, o2[:BLK], o2[BLK:])
            lse_ref[rows, cols] = jnp.where(low_lanes, lse2[:BLK], lse2[BLK:])


def _attn_group(qkv, dil):
    bn, _, _, length, _ = qkv.shape
    tq = min(length, 4 * BLK)
    per_tile = tq // BLK

    def cur(which):
        return pl.BlockSpec((None, None, None, tq, A_GROUP_W),
                            lambda b, r, n: (b, which, r, n, 0))

    def prev(which):
        return pl.BlockSpec((None, None, None, BLK, A_GROUP_W),
                            lambda b, r, n: (b, which, r, jnp.maximum(n * per_tile - 1, 0), 0))

    out = pl.BlockSpec((None, None, tq, A_GROUP_W), lambda b, r, n: (b, r, n, 0))
    return pl.pallas_call(
        _attn_kernel,
        grid=(bn, dil, length // tq),
        in_specs=[cur(0), prev(1), cur(1), prev(2), cur(2)],
        out_specs=[out, out],
        out_shape=[jax.ShapeDtypeStruct((bn, dil, length, A_GROUP_W), F32)] * 2,
        compiler_params=pltpu.CompilerParams(
            dimension_semantics=("arbitrary", "arbitrary", "arbitrary")),
        name=f"attn_d{dil}",
    )(qkv, qkv, qkv, qkv, qkv)


def _natural_order(ref, il_ref):
    dil, rows, _ = ref.shape
    if dil == 1:
        return ref[0]
    for r in range(dil):
        for slab in range(A_SLABS):
            il_ref[slab, pl.ds(r, rows, stride=dil), :] = (
                ref[r, :, slab * LANES:(slab + 1) * LANES])
    return jnp.concatenate([il_ref[slab] for slab in range(A_SLABS)], axis=1)


def _merge_kernel(x_ref, o0_ref, o1_ref, o2_ref, l0_ref, l1_ref, l2_ref, g0_ref, part_ref,
                  wpa_ref, wout_ref, g_ref, wg_ref, wu_ref, wd_ref, fin_ref, o_ref, il_ref,
                  *, final):
    l0, l1, l2 = (_natural_order(ref, il_ref.at[i])
                  for i, ref in enumerate((l0_ref, l1_ref, l2_ref)))
    mx = jnp.maximum(jnp.maximum(l0, l1), l2)
    e0, e1, e2 = jnp.exp(l0 - mx), jnp.exp(l1 - mx), jnp.exp(l2 - mx)
    o0, o1, o2 = (_natural_order(ref, il_ref.at[3 + i])
                  for i, ref in enumerate((o0_ref, o1_ref, o2_ref)))
    y_a = (e0 * o0 + e1 * o1 + e2 * o2) / (e0 + e1 + e2)
    merged = g0_ref[...] * jnp.dot(y_a.astype(BF16), wpa_ref[...], preferred_element_type=F32)
    merged = (merged + part_ref[...]).astype(BF16)
    x = x_ref[...] + jnp.dot(merged, wout_ref[...], preferred_element_type=F32)
    x = _swiglu_half_step(x, g_ref[...], wg_ref, wu_ref, wd_ref)
    if final:
        x = _rms_norm(x, fin_ref[...])
    o_ref[...] = x


def _merge(layer, x, os, lses, g0, part, wpa, wout, g, wg, wu, wd, fin, final):
    bn, s, _ = x.shape
    tm = TOKEN_TILE
    attn_specs = [pl.BlockSpec((None, dil, tm // dil, A_GROUP_W), lambda b, i: (b, 0, i, 0))
                  for _, dil in A_GROUPS]
    return pl.pallas_call(
        functools.partial(_merge_kernel, final=final),
        grid=(bn, s // tm),
        in_specs=[_token_spec(D_MODEL)] + attn_specs + attn_specs
        + [_token_spec(D_MODEL), _token_spec(D_MODEL),
           _layer_spec(layer, (A_GROUP_W, D_MODEL)), _layer_spec(layer, (D_MODEL, D_MODEL)),
           _layer_spec(layer, (1, D_MODEL)), _layer_spec(layer, (D_MODEL, D_FF)),
           _layer_spec(layer, (D_MODEL, D_FF)), _layer_spec(layer, (D_FF, D_MODEL)),
           _layer_spec(0, (1, D_MODEL))],
        out_specs=_token_spec(D_MODEL),
        out_shape=jax.ShapeDtypeStruct(x.shape, F32),
        scratch_shapes=[pltpu.VMEM((2 * len(A_GROUPS), A_SLABS, tm, LANES), F32)],
        compiler_params=pltpu.CompilerParams(
            dimension_semantics=("arbitrary", "arbitrary"),
            vmem_limit_bytes=VMEM_LIMIT_BYTES),
        name="merge_ffn",
    )(x, *os, *lses, g0, part, wpa, wout, g, wg, wu, wd, fin)


def kernel(x, ffn1_norm, ffn1_w_gate, ffn1_w_up, ffn1_w_down, mix_norm, w_in, b_gate,
           b_ln_g, b_ln_b, b_w_s, b_b_s, c_w, c_scale, w_proj_a, w_proj_b, w_proj_c,
           w_out, ffn2_norm, ffn2_w_gate, ffn2_w_up, ffn2_w_down, final_norm):
    bf = lambda w: w.astype(BF16)
    row = lambda p: p.reshape(p.shape[0], 1, -1)
    ffn1 = (row(ffn1_norm), bf(ffn1_w_gate), bf(ffn1_w_up), bf(ffn1_w_down))
    ffn2 = (row(ffn2_norm), bf(ffn2_w_gate), bf(ffn2_w_up), bf(ffn2_w_down))
    mix = (row(mix_norm), bf(w_in), row(b_gate), row(b_ln_g), row(b_ln_b), b_w_s,
           jnp.swapaxes(b_b_s, 1, 2), bf(c_w), row(c_scale), bf(w_proj_b), bf(w_proj_c))
    wpa, wout = bf(w_proj_a), bf(w_out)
    fin = final_norm.reshape(1, 1, -1)
    for layer in range(DEPTH):
        x = _ffn(layer, x, *ffn1)
        qkv0, qkv1, qkv2, g0, part = _inproj(layer, x, *mix)
        os, lses = zip(*(_attn_group(qkv, dil)
                         for qkv, (_, dil) in zip((qkv0, qkv1, qkv2), A_GROUPS)))
        x = _merge(layer, x, os, lses, g0, part, wpa, wout, *ffn2, fin,
                   final=(layer == DEPTH - 1))
    return x
```

```python
import functools
import math

import jax
import jax.numpy as jnp
from jax import lax
from jax.experimental import pallas as pl
from jax.experimental.pallas import tpu as pltpu

F32 = jnp.float32
BF16 = jnp.bfloat16

D_MODEL = 1024
D_FF = 2816
DEPTH = 2
EPS = 1e-6

LANES = 128
HEAD_DIM = 64
A_GROUPS = ((128, 1), (512, 4), (2048, 16))
A_HEADS = 4
A_GROUP_W = A_HEADS * HEAD_DIM
A_SLABS = A_GROUP_W // LANES
A_WIDTH = len(A_GROUPS) * A_GROUP_W
BLK = 128
B_CHUNK = 128
B_GROUPS = 4
B_WIDTH = 512
C_WINDOWS = (2, 4, 8, 16)
C_WIDTH = 512
C_GROUP_DIM = 128
C_HALO = 16

Q_OFF = 0
K_OFF = A_WIDTH
V_OFF = 2 * A_WIDTH
UB_OFF = 3 * A_WIDTH
VB_OFF = UB_OFF + B_WIDTH
C_OFF = VB_OFF + B_WIDTH
G_OFF = C_OFF + C_WIDTH
IN_WIDTH = G_OFF + 3 * D_MODEL

TOKEN_TILE = 512
VMEM_LIMIT_BYTES = 56 * 1024 * 1024
NEG_INF = -1e30


def _layer_spec(layer, shape):
    zeros = (0,) * len(shape)
    return pl.BlockSpec((None,) + shape, lambda *_: (layer,) + zeros,
                        pipeline_mode=pl.Buffered(1))


def _token_spec(width):
    return pl.BlockSpec((None, TOKEN_TILE, width), lambda b, i: (b, i, 0))


def _rms_norm(x, g):
    return x * lax.rsqrt(jnp.mean(x * x, axis=-1, keepdims=True) + EPS) * g


def _gelu(x):
    return 0.5 * x * (1.0 + lax.erf(x * math.sqrt(0.5)))


def _swiglu_half_step(xs, g, wg_ref, wu_ref, wd_ref):
    hs = [_rms_norm(x, g).astype(BF16) for x in xs]
    gates = [jnp.dot(h, wg_ref[...], preferred_element_type=F32) for h in hs]
    ups = [jnp.dot(h, wu_ref[...], preferred_element_type=F32) for h in hs]
    acts = [(gate * jax.nn.sigmoid(gate) * up).astype(BF16) for gate, up in zip(gates, ups)]
    return [x + 0.5 * jnp.dot(act, wd_ref[...], preferred_element_type=F32)
            for x, act in zip(xs, acts)]


def _row_groups(ref, n=2):
    rows = ref.shape[0] // n
    return [ref[i * rows:(i + 1) * rows, :] for i in range(n)]


def _store_row_groups(ref, values):
    rows = ref.shape[0] // len(values)
    for i, v in enumerate(values):
        ref[i * rows:(i + 1) * rows, :] = v


def _ffn_kernel(x_ref, g_ref, wg_ref, wu_ref, wd_ref, o_ref):
    _store_row_groups(
        o_ref, _swiglu_half_step(_row_groups(x_ref), g_ref[...], wg_ref, wu_ref, wd_ref))


def _ffn(layer, x, g, wg, wu, wd):
    bn, s, _ = x.shape
    return pl.pallas_call(
        _ffn_kernel,
        grid=(bn, s // TOKEN_TILE),
        in_specs=[_token_spec(D_MODEL), _layer_spec(layer, (1, D_MODEL)),
                  _layer_spec(layer, (D_MODEL, D_FF)), _layer_spec(layer, (D_MODEL, D_FF)),
                  _layer_spec(layer, (D_FF, D_MODEL))],
        out_specs=_token_spec(D_MODEL),
        out_shape=jax.ShapeDtypeStruct(x.shape, F32),
        compiler_params=pltpu.CompilerParams(
            dimension_semantics=("arbitrary", "arbitrary"),
            vmem_limit_bytes=VMEM_LIMIT_BYTES),
        name="ffn",
    )(x, g, wg, wu, wd)


def _inproj_kernel(x_ref, g_ref, win_ref, bgate_ref, lng_ref, lnb_ref, ws_ref, bs_ref,
                   cw_ref, cs_ref, wpb_ref, wpc_ref,
                   qkv0_ref, qkv1_ref, qkv2_ref, g0_ref, part_ref, cext_ref, dil_ref):
    tm = x_ref.shape[0]
    s_idx = pl.program_id(1)

    @pl.when(s_idx == 0)
    def _():
        cext_ref[0:C_HALO, :] = jnp.zeros((C_HALO, C_WIDTH), F32)

    h = _rms_norm(x_ref[...], g_ref[...]).astype(BF16)

    def proj(lo, hi):
        return jnp.dot(h, win_ref[:, lo:hi], preferred_element_type=F32)

    def gate_logits(i):
        lo = G_OFF + i * D_MODEL
        return proj(lo, lo + D_MODEL) + bgate_ref[:, i * D_MODEL:(i + 1) * D_MODEL]

    def attn_operand(which):
        off = (Q_OFF, K_OFF, V_OFF)[which]
        for gi, out_ref in enumerate((qkv0_ref, qkv1_ref, qkv2_ref)):
            dil = A_GROUPS[gi][1]
            z = proj(off + gi * A_GROUP_W, off + (gi + 1) * A_GROUP_W)
            if which == 0:
                z = z * (1.0 / math.sqrt(HEAD_DIM))
            if dil == 1:
                out_ref[which, 0] = z.astype(BF16)
                continue
            slabs = dil_ref.at[which * len(A_GROUPS) + gi]
            for slab in range(A_SLABS):
                slabs[slab] = z[:, slab * LANES:(slab + 1) * LANES]
            for r in range(dil):
                for slab in range(A_SLABS):
                    out_ref[which, r, :, slab * LANES:(slab + 1) * LANES] = (
                        slabs[slab, pl.ds(r, tm // dil, stride=dil), :].astype(BF16))

    def pooling(zc):
        cext_ref[C_HALO:C_HALO + tm, :] = zc
        t1 = (s_idx * tm + 1 + lax.broadcasted_iota(jnp.int32, (tm, 1), 0)).astype(F32)
        yc_groups = []
        for gi, w in enumerate(C_WINDOWS):
            cols = slice(gi * C_GROUP_DIM, (gi + 1) * C_GROUP_DIM)
            cur = zc[:, cols]
            acc = cur
            for j in range(1, w):
                acc = acc + cext_ref[C_HALO - j:C_HALO - j + tm, cols]
            pooled = (acc * (1.0 / jnp.minimum(t1, float(w))) - cur).astype(BF16)
            y = jnp.dot(pooled, cw_ref[gi], preferred_element_type=F32)
            yc_groups.append(y * cs_ref[:, cols])
        cext_ref[0:C_HALO, :] = zc[tm - C_HALO:tm, :]
        return jnp.concatenate(yc_groups, axis=1).astype(BF16)

    def spatial_gating(zv):
        vb = _gelu(zv)
        mu = jnp.mean(vb, axis=-1, keepdims=True)
        vc = vb - mu
        vn = vc * lax.rsqrt(jnp.mean(vc * vc, axis=-1, keepdims=True) + EPS)
        vn = (vn * lng_ref[...] + lnb_ref[...]).astype(BF16)
        row = lax.broadcasted_iota(jnp.int32, (B_CHUNK, B_CHUNK), 0)
        col = lax.broadcasted_iota(jnp.int32, (B_CHUNK, B_CHUNK), 1)
        causal = col <= row
        gd = B_WIDTH // B_GROUPS
        mixed_groups = []
        for gi in range(B_GROUPS):
            w = jnp.where(causal, ws_ref[gi], 0.0).astype(BF16)
            bias = bs_ref[:, gi:gi + 1]
            chunks = []
            for n in range(tm // B_CHUNK):
                vs = vn[n * B_CHUNK:(n + 1) * B_CHUNK, gi * gd:(gi + 1) * gd]
                chunks.append(jnp.dot(w, vs, preferred_element_type=F32) + bias)
            mixed_groups.append(jnp.concatenate(chunks, axis=0))
        return jnp.concatenate(mixed_groups, axis=1)

    z_local = proj(UB_OFF, G_OFF)
    attn_operand(0)
    z_g0 = gate_logits(0)
    y_c = pooling(z_local[:, 2 * B_WIDTH:])
    attn_operand(1)
    z_g1 = gate_logits(1)
    mixed = spatial_gating(z_local[:, B_WIDTH:2 * B_WIDTH])
    attn_operand(2)
    z_g2 = gate_logits(2)
    y_b = (_gelu(z_local[:, :B_WIDTH]) * mixed).astype(BF16)
    g0_ref[...] = jax.nn.sigmoid(z_g0)
    part = jax.nn.sigmoid(z_g1) * jnp.dot(y_b, wpb_ref[...], preferred_element_type=F32)
    part_ref[...] = part + jax.nn.sigmoid(z_g2) * jnp.dot(y_c, wpc_ref[...],
                                                         preferred_element_type=F32)


def _inproj(layer, x, g, win, bgate, lng, lnb, ws, bs_t, cw, cs, wpb, wpc):
    bn, s, _ = x.shape
    tm = TOKEN_TILE
    qkv_specs, qkv_shapes = [], []
    for _, dil in A_GROUPS:
        qkv_specs.append(pl.BlockSpec((None, 3, dil, tm // dil, A_GROUP_W),
                                      lambda b, i: (b, 0, 0, i, 0)))
        qkv_shapes.append(jax.ShapeDtypeStruct((bn, 3, dil, s // dil, A_GROUP_W), BF16))
    return pl.pallas_call(
        _inproj_kernel,
        grid=(bn, s // tm),
        in_specs=[_token_spec(D_MODEL), _layer_spec(layer, (1, D_MODEL)),
                  _layer_spec(layer, (D_MODEL, IN_WIDTH)), _layer_spec(layer, (1, 3 * D_MODEL)),
                  _layer_spec(layer, (1, B_WIDTH)), _layer_spec(layer, (1, B_WIDTH)),
                  _layer_spec(layer, (B_GROUPS, B_CHUNK, B_CHUNK)),
                  _layer_spec(layer, (B_CHUNK, B_GROUPS)),
                  _layer_spec(layer, (len(C_WINDOWS), C_GROUP_DIM, C_GROUP_DIM)),
                  _layer_spec(layer, (1, C_WIDTH)), _layer_spec(layer, (B_WIDTH, D_MODEL)),
                  _layer_spec(layer, (C_WIDTH, D_MODEL))],
        out_specs=qkv_specs + [_token_spec(D_MODEL), _token_spec(D_MODEL)],
        out_shape=qkv_shapes + [jax.ShapeDtypeStruct((bn, s, D_MODEL), F32)] * 2,
        scratch_shapes=[pltpu.VMEM((C_HALO + tm, C_WIDTH), F32),
                        pltpu.VMEM((3 * len(A_GROUPS), A_SLABS, tm, LANES), F32)],
        compiler_params=pltpu.CompilerParams(
            dimension_semantics=("arbitrary", "arbitrary"),
            vmem_limit_bytes=VMEM_LIMIT_BYTES),
        name="inproj",
    )(x, g, win, bgate, lng, lnb, ws, bs_t, cw, cs, wpb, wpc)


def _attn_kernel(q_ref, kp_ref, kc_ref, vp_ref, vc_ref, o_ref, lse_ref):
    tq = q_ref.shape[0]
    first_key = jnp.where(pl.program_id(2) == 0, BLK, 0)
    lane = lax.broadcasted_iota(jnp.int32, (BLK, 2 * HEAD_DIM), 1)
    low_lanes = lane < HEAD_DIM
    qi = lax.broadcasted_iota(jnp.int32, (2 * BLK, 2 * BLK), 0) % BLK
    kj = lax.broadcasted_iota(jnp.int32, (2 * BLK, 2 * BLK), 1)
    dist = qi + BLK - kj
    band = (dist >= 0) & (dist <= BLK)
    for j in range(tq // BLK):
        rows = slice(j * BLK, (j + 1) * BLK)
        valid = band
        if j == 0:
            valid = band & (kj >= first_key)
        for hp in range(A_HEADS // 2):
            cols = slice(hp * 2 * HEAD_DIM, (hp + 1) * 2 * HEAD_DIM)
            q2 = q_ref[rows, cols]
            zero = jnp.zeros_like(q2)
            qs = jnp.concatenate([jnp.where(low_lanes, q2, zero),
                                  jnp.where(low_lanes, zero, q2)], axis=0)
            if j == 0:
                kcat = jnp.concatenate([kp_ref[:, cols], kc_ref[rows, cols]], axis=0)
                vcat = jnp.concatenate([vp_ref[:, cols], vc_ref[rows, cols]], axis=0)
            else:
                both = slice((j - 1) * BLK, (j + 1) * BLK)
                kcat = kc_ref[both, cols]
                vcat = vc_ref[both, cols]
            s = lax.dot_general(qs, kcat, (((1,), (1,)), ((), ())),
                                preferred_element_type=F32)
            s = jnp.where(valid, s, NEG_INF)
            m = jnp.max(s, axis=-1, keepdims=True)
            p = jnp.exp(s - m)
            l = jnp.sum(p, axis=-1, keepdims=True)
            o2 = jnp.dot(p.astype(BF16), vcat, preferred_element_type=F32) / l
            lse2 = m + jnp.log(l)
            o_ref[rows, cols] = jnp.where(low_lanes, o2[:BLK], o2[BLK:])
            lse_ref[rows, cols] = jnp.where(low_lanes, lse2[:BLK], lse2[BLK:])


def _attn_group(qkv, dil):
    bn, _, _, length, _ = qkv.shape
    tq = min(length, 4 * BLK)
    per_tile = tq // BLK

    def cur(which):
        return pl.BlockSpec((None, None, None, tq, A_GROUP_W),
                            lambda b, r, n: (b, which, r, n, 0))

    def prev(which):
        return pl.BlockSpec((None, None, None, BLK, A_GROUP_W),
                            lambda b, r, n: (b, which, r, jnp.maximum(n * per_tile - 1, 0), 0))

    out = pl.BlockSpec((None, None, tq, A_GROUP_W), lambda b, r, n: (b, r, n, 0))
    return pl.pallas_call(
        _attn_kernel,
        grid=(bn, dil, length // tq),
        in_specs=[cur(0), prev(1), cur(1), prev(2), cur(2)],
        out_specs=[out, out],
        out_shape=[jax.ShapeDtypeStruct((bn, dil, length, A_GROUP_W), F32)] * 2,
        compiler_params=pltpu.CompilerParams(
            dimension_semantics=("arbitrary", "arbitrary", "arbitrary")),
        name=f"attn_d{dil}",
    )(qkv, qkv, qkv, qkv, qkv)


def _natural_order(ref, il_ref):
    dil, rows, _ = ref.shape
    if dil == 1:
        return ref[0]
    for r in range(dil):
        for slab in range(A_SLABS):
            il_ref[slab, pl.ds(r, rows, stride=dil), :] = (
                ref[r, :, slab * LANES:(slab + 1) * LANES])
    return jnp.concatenate([il_ref[slab] for slab in range(A_SLABS)], axis=1)


def _merge_kernel(x_ref, o0_ref, o1_ref, o2_ref, l0_ref, l1_ref, l2_ref, g0_ref, part_ref,
                  wpa_ref, wout_ref, g_ref, wg_ref, wu_ref, wd_ref, fin_ref, o_ref, il_ref,
                  *, final):
    l0, l1, l2 = (_natural_order(ref, il_ref.at[i])
                  for i, ref in enumerate((l0_ref, l1_ref, l2_ref)))
    mx = jnp.maximum(jnp.maximum(l0, l1), l2)
    e0, e1, e2 = jnp.exp(l0 - mx), jnp.exp(l1 - mx), jnp.exp(l2 - mx)
    o0, o1, o2 = (_natural_order(ref, il_ref.at[3 + i])
                  for i, ref in enumerate((o0_ref, o1_ref, o2_ref)))
    y_a = ((e0 * o0 + e1 * o1 + e2 * o2) / (e0 + e1 + e2)).astype(BF16)
    half = y_a.shape[0] // 2
    y_as = [y_a[:half], y_a[half:]]
    proj_a = [jnp.dot(y, wpa_ref[...], preferred_element_type=F32) for y in y_as]
    merged = [(g0 * pa + part).astype(BF16)
              for g0, pa, part in zip(_row_groups(g0_ref), proj_a, _row_groups(part_ref))]
    xs = [x + jnp.dot(m, wout_ref[...], preferred_element_type=F32)
          for x, m in zip(_row_groups(x_ref), merged)]
    xs = _swiglu_half_step(xs, g_ref[...], wg_ref, wu_ref, wd_ref)
    if final:
        xs = [_rms_norm(x, fin_ref[...]) for x in xs]
    _store_row_groups(o_ref, xs)


def _merge(layer, x, os, lses, g0, part, wpa, wout, g, wg, wu, wd, fin, final):
    bn, s, _ = x.shape
    tm = TOKEN_TILE
    attn_specs = [pl.BlockSpec((None, dil, tm // dil, A_GROUP_W), lambda b, i: (b, 0, i, 0))
                  for _, dil in A_GROUPS]
    return pl.pallas_call(
        functools.partial(_merge_kernel, final=final),
        grid=(bn, s // tm),
        in_specs=[_token_spec(D_MODEL)] + attn_specs + attn_specs
        + [_token_spec(D_MODEL), _token_spec(D_MODEL),
           _layer_spec(layer, (A_GROUP_W, D_MODEL)), _layer_spec(layer, (D_MODEL, D_MODEL)),
           _layer_spec(layer, (1, D_MODEL)), _layer_spec(layer, (D_MODEL, D_FF)),
           _layer_spec(layer, (D_MODEL, D_FF)), _layer_spec(layer, (D_FF, D_MODEL)),
           _layer_spec(0, (1, D_MODEL))],
        out_specs=_token_spec(D_MODEL),
        out_shape=jax.ShapeDtypeStruct(x.shape, F32),
        scratch_shapes=[pltpu.VMEM((2 * len(A_GROUPS), A_SLABS, tm, LANES), F32)],
        compiler_params=pltpu.CompilerParams(
            dimension_semantics=("arbitrary", "arbitrary"),
            vmem_limit_bytes=VMEM_LIMIT_BYTES),
        name="merge_ffn",
    )(x, *os, *lses, g0, part, wpa, wout, g, wg, wu, wd, fin)


def kernel(x, ffn1_norm, ffn1_w_gate, ffn1_w_up, ffn1_w_down, mix_norm, w_in, b_gate,
           b_ln_g, b_ln_b, b_w_s, b_b_s, c_w, c_scale, w_proj_a, w_proj_b, w_proj_c,
           w_out, ffn2_norm, ffn2_w_gate, ffn2_w_up, ffn2_w_down, final_norm):
    bf = lambda w: w.astype(BF16)
    row = lambda p: p.reshape(p.shape[0], 1, -1)
    ffn1 = (row(ffn1_norm), bf(ffn1_w_gate), bf(ffn1_w_up), bf(ffn1_w_down))
    ffn2 = (row(ffn2_norm), bf(ffn2_w_gate), bf(ffn2_w_up), bf(ffn2_w_down))
    mix = (row(mix_norm), bf(w_in), row(b_gate), row(b_ln_g), row(b_ln_b), b_w_s,
           jnp.swapaxes(b_b_s, 1, 2), bf(c_w), row(c_scale), bf(w_proj_b), bf(w_proj_c))
    wpa, wout = bf(w_proj_a), bf(w_out)
    fin = final_norm.reshape(1, 1, -1)
    for layer in range(DEPTH):
        x = _ffn(layer, x, *ffn1)
        qkv0, qkv1, qkv2, g0, part = _inproj(layer, x, *mix)
        os, lses = zip(*(_attn_group(qkv, dil)
                         for qkv, (_, dil) in zip((qkv0, qkv1, qkv2), A_GROUPS)))
        x = _merge(layer, x, os, lses, g0, part, wpa, wout, *ffn2, fin,
                   final=(layer == DEPTH - 1))
    return x
```

```python
import functools
import math

import jax
import jax.numpy as jnp
from jax import lax
from jax.experimental import pallas as pl
from jax.experimental.pallas import tpu as pltpu

F32 = jnp.float32
BF16 = jnp.bfloat16

D_MODEL = 1024
D_FF = 2816
DEPTH = 2
EPS = 1e-6

LANES = 128
HEAD_DIM = 64
A_GROUPS = ((128, 1), (512, 4), (2048, 16))
A_HEADS = 4
A_GROUP_W = A_HEADS * HEAD_DIM
A_SLABS = A_GROUP_W // LANES
A_WIDTH = len(A_GROUPS) * A_GROUP_W
BLK = 128
B_CHUNK = 128
B_GROUPS = 4
B_WIDTH = 512
C_WINDOWS = (2, 4, 8, 16)
C_WIDTH = 512
C_GROUP_DIM = 128
C_HALO = 16

Q_OFF = 0
K_OFF = A_WIDTH
V_OFF = 2 * A_WIDTH
UB_OFF = 3 * A_WIDTH
VB_OFF = UB_OFF + B_WIDTH
C_OFF = VB_OFF + B_WIDTH
G_OFF = C_OFF + C_WIDTH
IN_WIDTH = G_OFF + 3 * D_MODEL

TOKEN_TILE = 512
VMEM_LIMIT_BYTES = 56 * 1024 * 1024
NEG_INF = -1e30
LN2 = math.log(2.0)
Q_SCALE = math.log2(math.e) / math.sqrt(HEAD_DIM)


def _layer_spec(layer, shape):
    zeros = (0,) * len(shape)
    return pl.BlockSpec((None,) + shape, lambda *_: (layer,) + zeros,
                        pipeline_mode=pl.Buffered(1))


def _token_spec(width):
    return pl.BlockSpec((None, TOKEN_TILE, width), lambda b, i: (b, i, 0))


def _rms_norm(x, g):
    return x * lax.rsqrt(jnp.mean(x * x, axis=-1, keepdims=True) + EPS) * g


def _gelu(x):
    return 0.5 * x * (1.0 + lax.erf(x * math.sqrt(0.5)))


def _swiglu_half_step(xs, g, wg_ref, wu_ref, wd_ref):
    hs = [_rms_norm(x, g).astype(BF16) for x in xs]
    gates = [jnp.dot(h, wg_ref[...], preferred_element_type=F32) for h in hs]
    ups = [jnp.dot(h, wu_ref[...], preferred_element_type=F32) for h in hs]
    acts = [(gate * jax.nn.sigmoid(gate) * up).astype(BF16) for gate, up in zip(gates, ups)]
    return [x + 0.5 * jnp.dot(act, wd_ref[...], preferred_element_type=F32)
            for x, act in zip(xs, acts)]


def _row_groups(ref, n=2):
    rows = ref.shape[0] // n
    return [ref[i * rows:(i + 1) * rows, :] for i in range(n)]


def _store_row_groups(ref, values):
    rows = ref.shape[0] // len(values)
    for i, v in enumerate(values):
        ref[i * rows:(i + 1) * rows, :] = v


def _ffn_kernel(x_ref, g_ref, wg_ref, wu_ref, wd_ref, o_ref):
    _store_row_groups(
        o_ref, _swiglu_half_step(_row_groups(x_ref), g_ref[...], wg_ref, wu_ref, wd_ref))


def _ffn(layer, x, g, wg, wu, wd):
    bn, s, _ = x.shape
    return pl.pallas_call(
        _ffn_kernel,
        grid=(bn, s // TOKEN_TILE),
        in_specs=[_token_spec(D_MODEL), _layer_spec(layer, (1, D_MODEL)),
                  _layer_spec(layer, (D_MODEL, D_FF)), _layer_spec(layer, (D_MODEL, D_FF)),
                  _layer_spec(layer, (D_FF, D_MODEL))],
        out_specs=_token_spec(D_MODEL),
        out_shape=jax.ShapeDtypeStruct(x.shape, F32),
        compiler_params=pltpu.CompilerParams(
            dimension_semantics=("arbitrary", "arbitrary"),
            vmem_limit_bytes=VMEM_LIMIT_BYTES),
        name="ffn",
    )(x, g, wg, wu, wd)


def _inproj_kernel(x_ref, g_ref, win_ref, bgate_ref, lng_ref, lnb_ref, ws_ref, bs_ref,
                   cw_ref, cs_ref, wpb_ref, wpc_ref,
                   qkv0_ref, qkv1_ref, qkv2_ref, g0_ref, part_ref, cext_ref, dil_ref):
    tm = x_ref.shape[0]
    s_idx = pl.program_id(1)

    @pl.when(s_idx == 0)
    def _():
        cext_ref[0:C_HALO, :] = jnp.zeros((C_HALO, C_WIDTH), F32)

    h = _rms_norm(x_ref[...], g_ref[...]).astype(BF16)

    def proj(lo, hi):
        return jnp.dot(h, win_ref[:, lo:hi], preferred_element_type=F32)

    def gate_logits(i):
        lo = G_OFF + i * D_MODEL
        return proj(lo, lo + D_MODEL) + bgate_ref[:, i * D_MODEL:(i + 1) * D_MODEL]

    def attn_operand(which):
        off = (Q_OFF, K_OFF, V_OFF)[which]
        for gi, out_ref in enumerate((qkv0_ref, qkv1_ref, qkv2_ref)):
            dil = A_GROUPS[gi][1]
            z = proj(off + gi * A_GROUP_W, off + (gi + 1) * A_GROUP_W)
            if which == 0:
                z = z * Q_SCALE
            if dil == 1:
                out_ref[which, 0] = z.astype(BF16)
                continue
            slabs = dil_ref.at[which * len(A_GROUPS) + gi]
            for slab in range(A_SLABS):
                slabs[slab] = z[:, slab * LANES:(slab + 1) * LANES]
            for r in range(dil):
                for slab in range(A_SLABS):
                    out_ref[which, r, :, slab * LANES:(slab + 1) * LANES] = (
                        slabs[slab, pl.ds(r, tm // dil, stride=dil), :].astype(BF16))

    def pooling(zc):
        cext_ref[C_HALO:C_HALO + tm, :] = zc
        t1 = (s_idx * tm + 1 + lax.broadcasted_iota(jnp.int32, (tm, 1), 0)).astype(F32)
        yc_groups = []
        for gi, w in enumerate(C_WINDOWS):
            cols = slice(gi * C_GROUP_DIM, (gi + 1) * C_GROUP_DIM)
            cur = zc[:, cols]
            acc = cur
            for j in range(1, w):
                acc = acc + cext_ref[C_HALO - j:C_HALO - j + tm, cols]
            pooled = (acc * (1.0 / jnp.minimum(t1, float(w))) - cur).astype(BF16)
            y = jnp.dot(pooled, cw_ref[gi], preferred_element_type=F32)
            yc_groups.append(y * cs_ref[:, cols])
        cext_ref[0:C_HALO, :] = zc[tm - C_HALO:tm, :]
        return jnp.concatenate(yc_groups, axis=1).astype(BF16)

    def spatial_gating(zv):
        vb = _gelu(zv)
        mu = jnp.mean(vb, axis=-1, keepdims=True)
        vc = vb - mu
        vn = vc * lax.rsqrt(jnp.mean(vc * vc, axis=-1, keepdims=True) + EPS)
        vn = (vn * lng_ref[...] + lnb_ref[...]).astype(BF16)
        row = lax.broadcasted_iota(jnp.int32, (B_CHUNK, B_CHUNK), 0)
        col = lax.broadcasted_iota(jnp.int32, (B_CHUNK, B_CHUNK), 1)
        causal = col <= row
        gd = B_WIDTH // B_GROUPS
        mixed_groups = []
        for gi in range(B_GROUPS):
            w = jnp.where(causal, ws_ref[gi], 0.0).astype(BF16)
            bias = bs_ref[:, gi:gi + 1]
            chunks = []
            for n in range(tm // B_CHUNK):
                vs = vn[n * B_CHUNK:(n + 1) * B_CHUNK, gi * gd:(gi + 1) * gd]
                chunks.append(jnp.dot(w, vs, preferred_element_type=F32) + bias)
            mixed_groups.append(jnp.concatenate(chunks, axis=0))
        return jnp.concatenate(mixed_groups, axis=1)

    z_local = proj(UB_OFF, G_OFF)
    attn_operand(0)
    z_g0 = gate_logits(0)
    y_c = pooling(z_local[:, 2 * B_WIDTH:])
    attn_operand(1)
    z_g1 = gate_logits(1)
    mixed = spatial_gating(z_local[:, B_WIDTH:2 * B_WIDTH])
    attn_operand(2)
    z_g2 = gate_logits(2)
    y_b = (_gelu(z_local[:, :B_WIDTH]) * mixed).astype(BF16)
    g0_ref[...] = jax.nn.sigmoid(z_g0)
    part = jax.nn.sigmoid(z_g1) * jnp.dot(y_b, wpb_ref[...], preferred_element_type=F32)
    part_ref[...] = part + jax.nn.sigmoid(z_g2) * jnp.dot(y_c, wpc_ref[...],
                                                         preferred_element_type=F32)


def _inproj(layer, x, g, win, bgate, lng, lnb, ws, bs_t, cw, cs, wpb, wpc):
    bn, s, _ = x.shape
    tm = TOKEN_TILE
    qkv_specs, qkv_shapes = [], []
    for _, dil in A_GROUPS:
        qkv_specs.append(pl.BlockSpec((None, 3, dil, tm // dil, A_GROUP_W),
                                      lambda b, i: (b, 0, 0, i, 0)))
        qkv_shapes.append(jax.ShapeDtypeStruct((bn, 3, dil, s // dil, A_GROUP_W), BF16))
    return pl.pallas_call(
        _inproj_kernel,
        grid=(bn, s // tm),
        in_specs=[_token_spec(D_MODEL), _layer_spec(layer, (1, D_MODEL)),
                  _layer_spec(layer, (D_MODEL, IN_WIDTH)), _layer_spec(layer, (1, 3 * D_MODEL)),
                  _layer_spec(layer, (1, B_WIDTH)), _layer_spec(layer, (1, B_WIDTH)),
                  _layer_spec(layer, (B_GROUPS, B_CHUNK, B_CHUNK)),
                  _layer_spec(layer, (B_CHUNK, B_GROUPS)),
                  _layer_spec(layer, (len(C_WINDOWS), C_GROUP_DIM, C_GROUP_DIM)),
                  _layer_spec(layer, (1, C_WIDTH)), _layer_spec(layer, (B_WIDTH, D_MODEL)),
                  _layer_spec(layer, (C_WIDTH, D_MODEL))],
        out_specs=qkv_specs + [_token_spec(D_MODEL), _token_spec(D_MODEL)],
        out_shape=qkv_shapes + [jax.ShapeDtypeStruct((bn, s, D_MODEL), F32)] * 2,
        scratch_shapes=[pltpu.VMEM((C_HALO + tm, C_WIDTH), F32),
                        pltpu.VMEM((3 * len(A_GROUPS), A_SLABS, tm, LANES), F32)],
        compiler_params=pltpu.CompilerParams(
            dimension_semantics=("arbitrary", "arbitrary"),
            vmem_limit_bytes=VMEM_LIMIT_BYTES),
        name="inproj",
    )(x, g, win, bgate, lng, lnb, ws, bs_t, cw, cs, wpb, wpc)


ATTN_UNITS_PER_STEP = 32
ATTN_LOOKAHEAD = 3


def _attn_kernel(q_ref, kp_ref, kc_ref, vp_ref, vc_ref, o_ref, lse_ref):
    nres, tq, _ = q_ref.shape
    pair_w = 2 * HEAD_DIM
    first_key = jnp.where(pl.program_id(2) == 0, BLK, 0)
    low_lanes = lax.broadcasted_iota(jnp.int32, (BLK, pair_w), 1) < HEAD_DIM
    ones = jnp.ones((2 * BLK, pair_w), BF16)
    row = lax.broadcasted_iota(jnp.int32, (2 * BLK, BLK), 0)
    pos = lax.broadcasted_iota(jnp.int32, (2 * BLK, BLK), 1)
    one_hot = (row % BLK == pos).astype(BF16)
    dist = pos + BLK - row
    in_band = (dist >= 0) & (dist <= BLK)
    bias = jnp.where(in_band, 0.0, NEG_INF).astype(BF16)
    bias_first = jnp.where(in_band & (row >= first_key), 0.0, NEG_INF).astype(BF16)

    def keys_values(rr, j, cols):
        if j == 0:
            return (jnp.concatenate([kp_ref[rr, :, cols], kc_ref[rr, 0:BLK, cols]], axis=0),
                    jnp.concatenate([vp_ref[rr, :, cols], vc_ref[rr, 0:BLK, cols]], axis=0))
        both = slice((j - 1) * BLK, (j + 1) * BLK)
        return kc_ref[rr, both, cols], vc_ref[rr, both, cols]

    def scores(rr, j, hp):
        cols = slice(hp * pair_w, (hp + 1) * pair_w)
        q2 = q_ref[rr, j * BLK:(j + 1) * BLK, cols]
        zero = jnp.zeros_like(q2)
        qs = jnp.concatenate([jnp.where(low_lanes, q2, zero),
                              jnp.where(low_lanes, zero, q2)], axis=0)
        kcat, _ = keys_values(rr, j, cols)
        qs = jnp.concatenate([qs, one_hot], axis=1)
        kcat = jnp.concatenate([kcat, bias_first if j == 0 else bias], axis=1)
        return lax.dot_general(qs, kcat, (((1,), (1,)), ((), ())), preferred_element_type=F32)

    def finish(rr, j, hp, s):
        cols = slice(hp * pair_w, (hp + 1) * pair_w)
        rows = slice(j * BLK, (j + 1) * BLK)
        _, vcat = keys_values(rr, j, cols)
        m = jnp.max(s, axis=-1, keepdims=True)
        p = jnp.exp2(s - m).astype(BF16)
        ov = jnp.dot(p, jnp.concatenate([vcat, ones], axis=1), preferred_element_type=F32)
        acc = jnp.where(low_lanes, ov[:BLK, :pair_w], ov[BLK:, :pair_w])
        l = jnp.where(low_lanes, ov[:BLK, pair_w:], ov[BLK:, pair_w:])
        o_ref[rr, rows, cols] = acc / l
        lse_ref[rr, rows, cols] = jnp.where(low_lanes, m[:BLK], m[BLK:]) * LN2 + jnp.log(l)

    units = [(rr, j, hp) for rr in range(nres) for j in range(tq // BLK)
             for hp in range(A_HEADS // 2)]
    pending = [scores(*u) for u in units[:ATTN_LOOKAHEAD]]
    for k, unit in enumerate(units):
        if k + ATTN_LOOKAHEAD < len(units):
            pending.append(scores(*units[k + ATTN_LOOKAHEAD]))
        finish(*unit, pending.pop(0))


def _attn_group(qkv, dil):
    bn, _, _, length, _ = qkv.shape
    blocks = ATTN_UNITS_PER_STEP // (A_HEADS // 2)
    tq = min(length, blocks * BLK)
    nres = blocks * BLK // tq
    per_tile = tq // BLK

    def cur(which):
        return pl.BlockSpec((None, None, nres, tq, A_GROUP_W),
                            lambda b, r, n: (b, which, r, n, 0))

    def prev(which):
        return pl.BlockSpec((None, None, nres, BLK, A_GROUP_W),
                            lambda b, r, n: (b, which, r, jnp.maximum(n * per_tile - 1, 0), 0))

    out = pl.BlockSpec((None, nres, tq, A_GROUP_W), lambda b, r, n: (b, r, n, 0))
    return pl.pallas_call(
        _attn_kernel,
        grid=(bn, dil // nres, length // tq),
        in_specs=[cur(0), prev(1), cur(1), prev(2), cur(2)],
        out_specs=[out, out],
        out_shape=[jax.ShapeDtypeStruct((bn, dil, length, A_GROUP_W), F32)] * 2,
        compiler_params=pltpu.CompilerParams(
            dimension_semantics=("arbitrary", "arbitrary", "arbitrary")),
        name=f"attn_d{dil}",
    )(qkv, qkv, qkv, qkv, qkv)


def _natural_order(ref, il_ref):
    dil, rows, _ = ref.shape
    if dil == 1:
        return ref[0]
    for r in range(dil):
        for slab in range(A_SLABS):
            il_ref[slab, pl.ds(r, rows, stride=dil), :] = (
                ref[r, :, slab * LANES:(slab + 1) * LANES])
    return jnp.concatenate([il_ref[slab] for slab in range(A_SLABS)], axis=1)


def _merge_kernel(x_ref, o0_ref, o1_ref, o2_ref, l0_ref, l1_ref, l2_ref, g0_ref, part_ref,
                  wpa_ref, wout_ref, g_ref, wg_ref, wu_ref, wd_ref, fin_ref, o_ref, il_ref,
                  *, final):
    l0, l1, l2 = (_natural_order(ref, il_ref.at[i])
                  for i, ref in enumerate((l0_ref, l1_ref, l2_ref)))
    mx = jnp.maximum(jnp.maximum(l0, l1), l2)
    e0, e1, e2 = jnp.exp(l0 - mx), jnp.exp(l1 - mx), jnp.exp(l2 - mx)
    o0, o1, o2 = (_natural_order(ref, il_ref.at[3 + i])
                  for i, ref in enumerate((o0_ref, o1_ref, o2_ref)))
    y_a = ((e0 * o0 + e1 * o1 + e2 * o2) / (e0 + e1 + e2)).astype(BF16)
    half = y_a.shape[0] // 2
    y_as = [y_a[:half], y_a[half:]]
    proj_a = [jnp.dot(y, wpa_ref[...], preferred_element_type=F32) for y in y_as]
    merged = [(g0 * pa + part).astype(BF16)
              for g0, pa, part in zip(_row_groups(g0_ref), proj_a, _row_groups(part_ref))]
    xs = [x + jnp.dot(m, wout_ref[...], preferred_element_type=F32)
          for x, m in zip(_row_groups(x_ref), merged)]
    xs = _swiglu_half_step(xs, g_ref[...], wg_ref, wu_ref, wd_ref)
    if final:
        xs = [_rms_norm(x, fin_ref[...]) for x in xs]
    _store_row_groups(o_ref, xs)


def _merge(layer, x, os, lses, g0, part, wpa, wout, g, wg, wu, wd, fin, final):
    bn, s, _ = x.shape
    tm = TOKEN_TILE
    attn_specs = [pl.BlockSpec((None, dil, tm // dil, A_GROUP_W), lambda b, i: (b, 0, i, 0))
                  for _, dil in A_GROUPS]
    return pl.pallas_call(
        functools.partial(_merge_kernel, final=final),
        grid=(bn, s // tm),
        in_specs=[_token_spec(D_MODEL)] + attn_specs + attn_specs
        + [_token_spec(D_MODEL), _token_spec(D_MODEL),
           _layer_spec(layer, (A_GROUP_W, D_MODEL)), _layer_spec(layer, (D_MODEL, D_MODEL)),
           _layer_spec(layer, (1, D_MODEL)), _layer_spec(layer, (D_MODEL, D_FF)),
           _layer_spec(layer, (D_MODEL, D_FF)), _layer_spec(layer, (D_FF, D_MODEL)),
           _layer_spec(0, (1, D_MODEL))],
        out_specs=_token_spec(D_MODEL),
        out_shape=jax.ShapeDtypeStruct(x.shape, F32),
        scratch_shapes=[pltpu.VMEM((2 * len(A_GROUPS), A_SLABS, tm, LANES), F32)],
        compiler_params=pltpu.CompilerParams(
            dimension_semantics=("arbitrary", "arbitrary"),
            vmem_limit_bytes=VMEM_LIMIT_BYTES),
        name="merge_ffn",
    )(x, *os, *lses, g0, part, wpa, wout, g, wg, wu, wd, fin)


def kernel(x, ffn1_norm, ffn1_w_gate, ffn1_w_up, ffn1_w_down, mix_norm, w_in, b_gate,
           b_ln_g, b_ln_b, b_w_s, b_b_s, c_w, c_scale, w_proj_a, w_proj_b, w_proj_c,
           w_out, ffn2_norm, ffn2_w_gate, ffn2_w_up, ffn2_w_down, final_norm):
    bf = lambda w: w.astype(BF16)
    row = lambda p: p.reshape(p.shape[0], 1, -1)
    ffn1 = (row(ffn1_norm), bf(ffn1_w_gate), bf(ffn1_w_up), bf(ffn1_w_down))
    ffn2 = (row(ffn2_norm), bf(ffn2_w_gate), bf(ffn2_w_up), bf(ffn2_w_down))
    mix = (row(mix_norm), bf(w_in), row(b_gate), row(b_ln_g), row(b_ln_b), b_w_s,
           jnp.swapaxes(b_b_s, 1, 2), bf(c_w), row(c_scale), bf(w_proj_b), bf(w_proj_c))
    wpa, wout = bf(w_proj_a), bf(w_out)
    fin = final_norm.reshape(1, 1, -1)
    for layer in range(DEPTH):
        x = _ffn(layer, x, *ffn1)
        qkv0, qkv1, qkv2, g0, part = _inproj(layer, x, *mix)
        os, lses = zip(*(_attn_group(qkv, dil)
                         for qkv, (_, dil) in zip((qkv0, qkv1, qkv2), A_GROUPS)))
        x = _merge(layer, x, os, lses, g0, part, wpa, wout, *ffn2, fin,
                   final=(layer == DEPTH - 1))
    return x
```
